```python
import math
import jax, jax.numpy as jnp
from jax import lax
import numpy as np

D_MODEL = 1024
BATCH = 4
SEQ = 4096
DEPTH = 1

CHUNK = 64
D_MIX = D_MODEL
SB_HEADS = 8
SB_HEAD_DIM = 64
D_SB = SB_HEADS * SB_HEAD_DIM
CONV_GROUPS = 8
D_CONV = D_MIX - D_SB
CONV_WIDTH = 3
Q_BLOCK = 128
D_IN_PROJ = 3 * D_SB + 3 * D_CONV
PEER_HEADS = 8
N_KEYS = 128
N_EXPERTS = N_KEYS * N_KEYS
D_KEY = 256
D_HALF = D_KEY // 2
TOPK_HALF = 16
TOPK = 16
TOKEN_BLOCK = 128
EPS = 1e-6

kernel_name = "hymba_stickbreak_shortconv_peer_block"


def _rmsnorm(x, g):
    x32 = x.astype(jnp.float32)
    inv = lax.rsqrt(jnp.mean(x32 * x32, axis=-1, keepdims=True) + EPS)
    return (x32 * inv).astype(x.dtype) * g


def _stick_breaking(q, k, v):
    seq = q.shape[2]
    scale = 1.0 / math.sqrt(q.shape[-1])
    outs = []
    for i in range(seq // Q_BLOCK):
        q0 = i * Q_BLOCK
        kv_len = q0 + Q_BLOCK
        qb = q[:, :, q0:kv_len].astype(jnp.float32)
        kb = k[:, :, :kv_len].astype(jnp.float32)
        vb = v[:, :, :kv_len]
        z = jnp.einsum('bhqd,bhkd->bhqk', qb, kb) * scale
        t_pos = q0 + jnp.arange(Q_BLOCK)
        s_pos = jnp.arange(kv_len)
        mask = s_pos[None, :] < t_pos[:, None]
        log_beta = jax.nn.log_sigmoid(z)
        log_1m = jnp.where(mask, jax.nn.log_sigmoid(-z), 0.0)
        log_stick = lax.cumsum(log_1m, axis=3, reverse=True) - log_1m
        a = jnp.where(mask, jnp.exp(log_beta + log_stick), 0.0)
        outs.append(jnp.einsum('bhqk,bhkd->bhqd', a.astype(vb.dtype), vb))
    return jnp.concatenate(outs, axis=2)


def _short_conv(xc, b_gate, c_gate, conv_w):
    seq = xc.shape[1]
    u = c_gate * xc
    up = jnp.pad(u, ((0, 0), (CONV_WIDTH - 1, 0), (0, 0)))
    y = sum(conv_w[w] * up[:, w:w + seq] for w in range(CONV_WIDTH))
    return b_gate * y


def _peer(h, w_q, sub_keys, expert_u, expert_v):
    bsz, seq, d = h.shape
    tok = h.reshape(bsz * seq, d)
    q = (tok @ w_q).reshape(-1, PEER_HEADS, D_KEY)
    q1, q2 = q[..., :D_HALF], q[..., D_HALF:]
    s1 = jnp.einsum('thd,kd->thk', q1, sub_keys[0]).astype(jnp.float32)
    s2 = jnp.einsum('thd,kd->thk', q2, sub_keys[1]).astype(jnp.float32)
    v1, i1 = lax.top_k(s1, TOPK_HALF)
    v2, i2 = lax.top_k(s2, TOPK_HALF)
    cand_s = (v1[..., :, None] + v2[..., None, :]).reshape(-1, PEER_HEADS, TOPK_HALF * TOPK_HALF)
    cand_i = (i1[..., :, None] * N_KEYS + i2[..., None, :]).reshape(-1, PEER_HEADS, TOPK_HALF * TOPK_HALF)
    top_s, pos = lax.top_k(cand_s, TOPK)
    idx = jnp.take_along_axis(cand_i, pos, axis=-1)
    gate = jax.nn.softmax(top_s, axis=-1).astype(h.dtype)

    n_blk = tok.shape[0] // TOKEN_BLOCK
    xb = tok.reshape(n_blk, TOKEN_BLOCK, d)
    ib = idx.reshape(n_blk, TOKEN_BLOCK, PEER_HEADS, TOPK)
    gb = gate.reshape(n_blk, TOKEN_BLOCK, PEER_HEADS, TOPK)

    def block_fn(args):
        xt, it, gt = args
        u = expert_u[it]
        vv = expert_v[it]
        act = jax.nn.gelu(jnp.einsum('thkd,td->thk', u, xt), approximate=False)
        return jnp.einsum('thk,thkd->td', gt * act, vv)

    out = lax.map(block_fn, (xb, ib, gb))
    return out.reshape(bsz, seq, d)


def setup_inputs(seed: int = 0) -> dict:
    key = jax.random.key(seed)
    ks = jax.random.split(key, 16)
    f32 = jnp.float32

    def gain(k, n):
        return (1.0 + 0.02 * jax.random.normal(k, (DEPTH, n))).astype(f32)

    return {
        "x": jax.random.normal(ks[0], (BATCH, SEQ, D_MODEL), f32),
        "norm1_g": gain(ks[1], D_MODEL),
        "w_in": jax.random.normal(ks[2], (DEPTH, D_MODEL, D_IN_PROJ), f32) * D_MODEL ** -0.5,
        "q_norm_g": gain(ks[3], SB_HEAD_DIM),
        "k_norm_g": gain(ks[4], SB_HEAD_DIM),
        "conv_w": jax.random.normal(ks[5], (DEPTH, CONV_WIDTH, D_CONV), f32) * CONV_WIDTH ** -0.5,
        "sb_out_g": gain(ks[6], D_SB),
        "conv_out_g": gain(ks[7], D_CONV),
        "w_out": jax.random.normal(ks[8], (DEPTH, D_MIX, D_MODEL), f32) * D_MIX ** -0.5,
        "norm2_g": gain(ks[9], D_MODEL),
        "peer_w_q": jax.random.normal(ks[10], (DEPTH, D_MODEL, PEER_HEADS * D_KEY), f32) * D_MODEL ** -0.5,
        "peer_sub_keys": jax.random.normal(ks[11], (DEPTH, 2, N_KEYS, D_HALF), f32) * D_HALF ** -0.5,
        "peer_u": jax.random.normal(ks[12], (DEPTH, N_EXPERTS, D_MODEL), f32) * D_MODEL ** -0.5,
        "peer_v": jax.random.normal(ks[13], (DEPTH, N_EXPERTS, D_MODEL), f32) * 0.5,
    }


def reference(x, norm1_g, w_in, q_norm_g, k_norm_g, conv_w, sb_out_g, conv_out_g,
              w_out, norm2_g, peer_w_q, peer_sub_keys, peer_u, peer_v):
    bsz, seq, _ = x.shape
    for l in range(DEPTH):
        h = _rmsnorm(x, norm1_g[l])
        proj = h @ w_in[l]
        q, k, v, xc, b_gate, c_gate = jnp.split(
            proj, [D_SB, 2 * D_SB, 3 * D_SB, 3 * D_SB + D_CONV, 3 * D_SB + 2 * D_CONV], axis=-1)
        q = _rmsnorm(q.reshape(bsz, seq, SB_HEADS, SB_HEAD_DIM), q_norm_g[l]).transpose(0, 2, 1, 3)
        k = _rmsnorm(k.reshape(bsz, seq, SB_HEADS, SB_HEAD_DIM), k_norm_g[l]).transpose(0, 2, 1, 3)
        v = v.reshape(bsz, seq, SB_HEADS, SB_HEAD_DIM).transpose(0, 2, 1, 3)
        sb = _stick_breaking(q, k, v).transpose(0, 2, 1, 3).reshape(bsz, seq, D_SB)
        cv = _short_conv(xc, b_gate, c_gate, conv_w[l])
        mixed = jnp.concatenate([_rmsnorm(sb, sb_out_g[l]), _rmsnorm(cv, conv_out_g[l])], axis=-1)
        x = x + mixed @ w_out[l]
        h2 = _rmsnorm(x, norm2_g[l])
        x = x + _peer(h2, peer_w_q[l], peer_sub_keys[l], peer_u[l], peer_v[l])
    return x
```

```python
import functools
import math

import jax
import jax.numpy as jnp
from jax import lax
from jax.experimental import pallas as pl
from jax.experimental.pallas import tpu as pltpu

F32 = jnp.float32
BF16 = jnp.bfloat16

EPS = 1e-6
SB_HEADS = 8
SB_HEAD_DIM = 64
D_SB = SB_HEADS * SB_HEAD_DIM
D_CONV = 512
PEER_HEADS = 8
N_KEYS = 128
D_HALF = 128
TOPK = 16
LANES = 128
VMEM_LIMIT = 56 * 1024 * 1024

_NT = (((1,), (1,)), ((), ()))


def _dot(a, b):
    return jnp.dot(a, b, preferred_element_type=F32)


def _dot_hilo(a, b):
    hi = a.astype(BF16)
    lo = (a - hi.astype(F32)).astype(BF16)
    return _dot(hi, b) + _dot(lo, b)


def _rms_scale(x):
    return lax.rsqrt(jnp.mean(x * x, axis=-1, keepdims=True) + EPS)


def _in_proj_kernel(x_ref, g_ref, w_ref, bd_ref, qg_ref, kg_ref,
                    q_ref, k_ref, v_ref, u_ref, b_ref):
    x = x_ref[...]
    h = (x * _rms_scale(x) * g_ref[...]).astype(BF16)
    proj = _dot(h, w_ref[...])
    q = proj[:, 0:D_SB]
    k = proj[:, D_SB:2 * D_SB]
    bd = bd_ref[...]
    q_ms = _dot_hilo(q * q, bd)
    k_ms = _dot_hilo(k * k, bd)
    scale = 1.0 / math.sqrt(SB_HEAD_DIM)
    q_ref[...] = (q * lax.rsqrt(q_ms + EPS) * qg_ref[...] * scale).astype(BF16)
    k_ref[...] = (k * lax.rsqrt(k_ms + EPS) * kg_ref[...]).astype(BF16)
    v_ref[...] = proj[:, 2 * D_SB:3 * D_SB].astype(BF16)
    xc = proj[:, 3 * D_SB:3 * D_SB + D_CONV]
    u_ref[...] = proj[:, 3 * D_SB + 2 * D_CONV:3 * D_SB + 3 * D_CONV] * xc
    b_ref[...] = proj[:, 3 * D_SB + D_CONV:3 * D_SB + 2 * D_CONV]


def _in_proj(x2, g1, w_in, q_g, k_g, tm):
    t, d = x2.shape
    n = w_in.shape[1]
    head = jnp.arange(D_SB) // SB_HEAD_DIM
    bd = jnp.where(head[:, None] == head[None, :], 1.0 / SB_HEAD_DIM, 0.0).astype(BF16)
    row = lambda i: (i, 0)
    fixed = lambda i: (0, 0)
    out = jax.ShapeDtypeStruct
    return pl.pallas_call(
        _in_proj_kernel,
        grid=(t // tm,),
        in_specs=[
            pl.BlockSpec((tm, d), row),
            pl.BlockSpec((1, d), fixed),
            pl.BlockSpec((d, n), fixed),
            pl.BlockSpec((D_SB, D_SB), fixed),
            pl.BlockSpec((1, D_SB), fixed),
            pl.BlockSpec((1, D_SB), fixed),
        ],
        out_specs=[pl.BlockSpec((tm, D_SB), row)] * 3 + [pl.BlockSpec((tm, D_CONV), row)] * 2,
        out_shape=[out((t, D_SB), BF16)] * 3 + [out((t, D_CONV), F32)] * 2,
        compiler_params=pltpu.CompilerParams(
            dimension_semantics=("arbitrary",), vmem_limit_bytes=VMEM_LIMIT),
        name="in_proj",
    )(x2, g1, w_in.astype(BF16), bd,
      jnp.tile(q_g, (1, SB_HEADS)), jnp.tile(k_g, (1, SB_HEADS)))


def _attn_kernel(q_ref, k_ref, v_ref, tri_ref, o_ref, *, blk):
    qi = pl.program_id(1)
    tri = tri_ref[...]
    lane = lax.broadcasted_iota(jnp.int32, (blk, LANES), 1)
    r_io = lax.broadcasted_iota(jnp.int32, (blk, blk), 0)
    c_io = lax.broadcasted_iota(jnp.int32, (blk, blk), 1)
    causal = c_io < r_io

    def step(q_h, pair, kj, carry, acc, diagonal):
        rows = pl.ds(pl.multiple_of(kj * blk, blk), blk)
        k_blk = k_ref[0, rows, pair * LANES:(pair + 1) * LANES]
        v_blk = v_ref[0, rows, pair * LANES:(pair + 1) * LANES]
        z = lax.dot_general(q_h, k_blk, _NT, preferred_element_type=F32)
        soft = jnp.log1p(jnp.exp(-jnp.abs(z)))
        log_beta = jnp.minimum(z, 0.0) - soft
        log_1m = log_beta - z
        if diagonal:
            log_1m = jnp.where(causal, log_1m, 0.0)
        cum = _dot_hilo(log_1m, tri)
        a = jnp.exp(log_beta + carry + cum[:, :blk])
        if diagonal:
            a = jnp.where(causal, a, 0.0)
        acc = acc + _dot(a.astype(BF16), v_blk)
        return carry + cum[:, blk:], acc

    for pair in range(SB_HEADS // 2):
        q_pair = q_ref[0, :, pair * LANES:(pair + 1) * LANES]
        outs = []
        for half in range(2):
            in_head = (lane < SB_HEAD_DIM) if half == 0 else (lane >= SB_HEAD_DIM)
            q_h = jnp.where(in_head, q_pair, jnp.zeros_like(q_pair))
            zero = jnp.zeros((blk, LANES), F32)
            carry, acc = step(q_h, pair, qi, zero, zero, True)

            def body(j, st, q_h=q_h, pair=pair):
                return step(q_h, pair, qi - 1 - j, st[0], st[1], False)

            carry, acc = lax.fori_loop(0, qi, body, (carry, acc))
            outs.append(acc)
        o_ref[0, :, pair * LANES:(pair + 1) * LANES] = jnp.where(
            lane < SB_HEAD_DIM, outs[0], outs[1])


def _attention(q, k, v, blk):
    b, s, d = q.shape
    j_io = jnp.arange(blk)
    tri = jnp.concatenate(
        [(j_io[:, None] > j_io[None, :]).astype(BF16), jnp.ones((blk, blk), BF16)], axis=1)
    return pl.pallas_call(
        functools.partial(_attn_kernel, blk=blk),
        grid=(b, s // blk),
        in_specs=[
            pl.BlockSpec((1, blk, d), lambda bi, qi: (bi, qi, 0)),
            pl.BlockSpec((1, s, d), lambda bi, qi: (bi, 0, 0)),
            pl.BlockSpec((1, s, d), lambda bi, qi: (bi, 0, 0)),
            pl.BlockSpec((blk, 2 * blk), lambda bi, qi: (0, 0)),
        ],
        out_specs=pl.BlockSpec((1, blk, d), lambda bi, qi: (bi, qi, 0)),
        out_shape=jax.ShapeDtypeStruct((b, s, d), F32),
        compiler_params=pltpu.CompilerParams(
            dimension_semantics=("arbitrary", "arbitrary"), vmem_limit_bytes=VMEM_LIMIT),
        name="attn",
    )(q, k, v, tri)


def _mix_out_kernel(sb_ref, u_ref, up_ref, b_ref, x_ref, cw_ref, sg_ref, cg_ref, wo_ref,
                    g2_ref, wq_ref, keys_ref, x1_ref, h2_ref, s_ref, *, tm, seq):
    i = pl.program_id(0)
    u = u_ref[...]
    prev = up_ref[0]
    prev = jnp.where((i * tm) % seq == 0, jnp.zeros_like(prev), prev)
    row = lax.broadcasted_iota(jnp.int32, u.shape, 0)
    p1 = jnp.broadcast_to(prev[7:8, :], u.shape)
    p2 = jnp.broadcast_to(prev[6:7, :], u.shape)
    u1 = jnp.where(row == 0, p1, pltpu.roll(u, 1, 0))
    u2 = jnp.where(row == 0, p2, jnp.where(row == 1, p1, pltpu.roll(u, 2, 0)))
    cw = cw_ref[...]
    cv = b_ref[...] * (cw[0:1, :] * u2 + cw[1:2, :] * u1 + cw[2:3, :] * u)
    sb = sb_ref[...]
    mixed = jnp.concatenate(
        [sb * _rms_scale(sb) * sg_ref[...], cv * _rms_scale(cv) * cg_ref[...]], axis=-1)
    x1 = x_ref[...] + _dot(mixed.astype(BF16), wo_ref[...])
    x1_ref[...] = x1
    h2 = (x1 * _rms_scale(x1) * g2_ref[...]).astype(BF16)
    h2_ref[...] = h2
    qp = _dot(h2, wq_ref[...]).astype(BF16)
    for c in range(2 * PEER_HEADS):
        keys = keys_ref[c % 2]
        s_ref[c] = lax.dot_general(keys, qp[:, c * D_HALF:(c + 1) * D_HALF], _NT,
                                   preferred_element_type=F32)


def _mix_out(sb, u, bgate, x2, conv_w, sb_g, cv_g, w_out, g2, w_q, sub_keys, tm, seq):
    t, d = x2.shape
    nq = w_q.shape[1]
    row = lambda i: (i, 0)
    fixed = lambda i: (0, 0)
    out = jax.ShapeDtypeStruct
    return pl.pallas_call(
        functools.partial(_mix_out_kernel, tm=tm, seq=seq),
        grid=(t // tm,),
        in_specs=[
            pl.BlockSpec((tm, D_SB), row),
            pl.BlockSpec((tm, D_CONV), row),
            pl.BlockSpec((1, 8, D_CONV), lambda i: (jnp.maximum(i * (tm // 8) - 1, 0), 0, 0)),
            pl.BlockSpec((tm, D_CONV), row),
            pl.BlockSpec((tm, d), row),
            pl.BlockSpec((3, D_CONV), fixed),
            pl.BlockSpec((1, D_SB), fixed),
            pl.BlockSpec((1, D_CONV), fixed),
            pl.BlockSpec((d, d), fixed),
            pl.BlockSpec((1, d), fixed),
            pl.BlockSpec((d, nq), fixed),
            pl.BlockSpec((2, N_KEYS, D_HALF), lambda i: (0, 0, 0)),
        ],
        out_specs=[
            pl.BlockSpec((tm, d), row),
            pl.BlockSpec((tm, d), row),
            pl.BlockSpec((2 * PEER_HEADS, N_KEYS, tm), lambda i: (0, 0, i)),
        ],
        out_shape=[out((t, d), F32), out((t, d), BF16),
                   out((2 * PEER_HEADS, N_KEYS, t), F32)],
        compiler_params=pltpu.CompilerParams(
            dimension_semantics=("arbitrary",), vmem_limit_bytes=VMEM_LIMIT),
        name="mix_out",
    )(sb, u, u.reshape(t // 8, 8, D_CONV), bgate, x2, conv_w, sb_g, cv_g,
      w_out.astype(BF16), g2, w_q.astype(BF16), sub_keys.astype(BF16))


def _extract_top(scores, n_rows):
    tl = scores.shape[1]
    r_io = lax.broadcasted_iota(jnp.int32, (n_rows, tl), 0)
    k_io = lax.broadcasted_iota(jnp.int32, (TOPK, tl), 0)

    def body(r, st):
        work, rank, vals = st
        m = jnp.max(work, axis=0, keepdims=True)
        first = jnp.min(jnp.where(work == m, r_io, n_rows), axis=0, keepdims=True)
        sel = r_io == first
        return (jnp.where(sel, -jnp.inf, work), jnp.where(sel, r, rank),
                jnp.where(k_io == r, jnp.broadcast_to(m, (TOPK, tl)), vals))

    init = (scores, jnp.full((n_rows, tl), n_rows, jnp.int32), jnp.zeros((TOPK, tl), F32))
    _, rank, vals = lax.fori_loop(0, TOPK, body, init)
    return rank, vals


def _peer_topk_kernel(s_ref, r2_ref, c1_ref, e1_ref, e2_ref):
    s1 = s_ref[0]
    s2 = s_ref[1]
    tl = s1.shape[1]
    rank1, v1 = _extract_top(s1, N_KEYS)
    rank2, v2 = _extract_top(s2, N_KEYS)
    cand = jnp.concatenate([v1[a:a + 1, :] + v2 for a in range(TOPK)], axis=0)
    rank3, _ = _extract_top(cand, TOPK * TOPK)
    sel3 = rank3 < TOPK
    top = v1[0:1, :] + v2[0:1, :]
    z = jnp.sum(jnp.where(sel3, jnp.exp(cand - top), 0.0), axis=0, keepdims=True)
    cnt1 = jnp.zeros((N_KEYS, tl), F32)
    for a in range(TOPK):
        nb = jnp.sum(sel3[a * TOPK:(a + 1) * TOPK, :].astype(F32), axis=0, keepdims=True)
        cnt1 = cnt1 + jnp.where(rank1 == a, nb, 0.0)
    r2_ref[0] = rank2.astype(F32).astype(BF16)
    c1_ref[0] = cnt1
    e1_ref[0] = jnp.exp(s1 - v1[0:1, :])
    e2_ref[0] = (jnp.exp(s2 - v2[0:1, :]) * (0.5 / z)).astype(BF16)


def _peer_topk(scores, tl):
    c, n, t = scores.shape
    spec = pl.BlockSpec((1, n, tl), lambda i, h: (h, 0, i))
    out = jax.ShapeDtypeStruct
    return pl.pallas_call(
        _peer_topk_kernel,
        grid=(t // tl, PEER_HEADS),
        in_specs=[pl.BlockSpec((2, n, tl), lambda i, h: (h, 0, i))],
        out_specs=[spec] * 4,
        out_shape=[out((PEER_HEADS, n, t), BF16), out((PEER_HEADS, n, t), F32),
                   out((PEER_HEADS, n, t), F32), out((PEER_HEADS, n, t), BF16)],
        compiler_params=pltpu.CompilerParams(
            dimension_semantics=("arbitrary", "arbitrary"), vmem_limit_bytes=VMEM_LIMIT),
        name="peer_topk",
    )(scores)


def _peer_ffn_kernel(h2_ref, u_ref, vt_ref, r2_ref, c1_ref, e1_ref, e2_ref, x1_ref,
                     o_ref, acc_ref, *, ne):
    j = pl.program_id(1)
    tm = h2_ref.shape[0]

    @pl.when(j == 0)
    def _():
        acc_ref[...] = jnp.zeros_like(acc_ref)

    act = lax.dot_general(u_ref[...], h2_ref[...], _NT, preferred_element_type=F32)
    act = act * (1.0 + lax.erf(act * (1.0 / math.sqrt(2.0))))
    n_sub = ne // N_KEYS
    gates = []
    for c in range(n_sub):
        i1 = j * n_sub + c
        g = jnp.zeros((N_KEYS, tm), BF16)
        for h in range(PEER_HEADS):
            cnt = jnp.broadcast_to(c1_ref[h, pl.ds(i1, 1), :], (N_KEYS, tm)).astype(BF16)
            w1 = jnp.broadcast_to(e1_ref[h, pl.ds(i1, 1), :], (N_KEYS, tm)).astype(BF16)
            g = g + jnp.where(r2_ref[h] < cnt, e2_ref[h], jnp.zeros_like(g)) * w1
        gates.append(g)
    p = act.astype(BF16) * jnp.concatenate(gates, axis=0)
    acc_ref[...] += _dot(vt_ref[...], p)

    @pl.when(j == pl.num_programs(1) - 1)
    def _():
        o_ref[...] = x1_ref[...] + acc_ref[...].T


def _peer_ffn(h2, u_tab, vt_tab, r2, c1, e1, e2, x1, tm, ne):
    t, d = h2.shape
    n_exp = u_tab.shape[0]
    tab = pl.BlockSpec((PEER_HEADS, N_KEYS, tm), lambda i, j: (0, 0, i))
    return pl.pallas_call(
        functools.partial(_peer_ffn_kernel, ne=ne),
        grid=(t // tm, n_exp // ne),
        in_specs=[
            pl.BlockSpec((tm, d), lambda i, j: (i, 0)),
            pl.BlockSpec((ne, d), lambda i, j: (j, 0)),
            pl.BlockSpec((d, ne), lambda i, j: (0, j)),
            tab, tab, tab, tab,
            pl.BlockSpec((tm, d), lambda i, j: (i, 0)),
        ],
        out_specs=pl.BlockSpec((tm, d), lambda i, j: (i, 0)),
        out_shape=jax.ShapeDtypeStruct((t, d), F32),
        scratch_shapes=[pltpu.VMEM((d, tm), F32)],
        compiler_params=pltpu.CompilerParams(
            dimension_semantics=("arbitrary", "arbitrary"), vmem_limit_bytes=VMEM_LIMIT),
        name="peer_ffn",
    )(h2, u_tab, vt_tab, r2, c1, e1, e2, x1)


def kernel(x, norm1_g, w_in, q_norm_g, k_norm_g, conv_w, sb_out_g, conv_out_g, w_out,
           norm2_g, peer_w_q, peer_sub_keys, peer_u, peer_v):
    bsz, seq, d = x.shape
    t = bsz * seq
    x2 = x.reshape(t, d)
    for l in range(norm1_g.shape[0]):
        q, k, v, u, bgate = _in_proj(x2, norm1_g[l:l + 1], w_in[l], q_norm_g[l:l + 1],
                                     k_norm_g[l:l + 1], tm=min(512, t))
        sb = _attention(q.reshape(bsz, seq, D_SB), k.reshape(bsz, seq, D_SB),
                        v.reshape(bsz, seq, D_SB), blk=128)
        x1, h2, scores = _mix_out(sb.reshape(t, D_SB), u, bgate, x2, conv_w[l],
                                  sb_out_g[l:l + 1], conv_out_g[l:l + 1], w_out[l],
                                  norm2_g[l:l + 1], peer_w_q[l], peer_sub_keys[l],
                                  tm=min(256, t), seq=seq)
        r2, c1, e1, e2 = _peer_topk(scores, tl=min(256, t))
        x2 = _peer_ffn(h2, peer_u[l].astype(BF16), peer_v[l].T.astype(BF16),
                       r2, c1, e1, e2, x1, tm=min(512, t), ne=512)
    return x2.reshape(bsz, seq, d)
```
